```python
import math
import jax, jax.numpy as jnp
from jax import lax
import numpy as np

D_MODEL = 2048
BATCH = 8
SEQ = 4096
DEPTH = 4

N_MEM = 256
HEAD_DIM = 128
MIX_WIDTH = D_MODEL
MEM_HEADS = 4
MEM_WIDTH = MEM_HEADS * HEAD_DIM
LOCAL_WIDTH = MIX_WIDTH - MEM_WIDTH
POOL_WINDOWS = (2, 4, 8, 16)
POOL_GROUP = LOCAL_WIDTH // len(POOL_WINDOWS)
DIL_PATTERNS = ((128, 1), (512, 4), (2048, 16))
DIL_HEADS_PER_GROUP = LOCAL_WIDTH // HEAD_DIM // len(DIL_PATTERNS)
DIL_GROUP_WIDTH = DIL_HEADS_PER_GROUP * HEAD_DIM
Q_BLOCK = 128
N_MIXERS = 2
N_POOL_LAYERS = (DEPTH + 1) // 2
N_DIL_LAYERS = DEPTH // 2
PEER_HEADS = 8
PEER_N_KEYS = 128
PEER_EXPERTS = PEER_N_KEYS * PEER_N_KEYS
PEER_TOPK = 16
PEER_QDIM = 256
PEER_HALF = PEER_QDIM // 2
PEER_CHUNK = 128
DEEPNORM_ALPHA = (2 * DEPTH) ** 0.25
DEEPNORM_BETA = (8 * DEPTH) ** -0.25
LN_EPS = 1e-5
NEG = -1e30

kernel_name = "hybrid_pool_dilated_peer_deepnorm"


def layer_norm(z, g, b):
    zf = z.astype(jnp.float32)
    mu = jnp.mean(zf, axis=-1, keepdims=True)
    var = jnp.mean(jnp.square(zf - mu), axis=-1, keepdims=True)
    return ((zf - mu) * lax.rsqrt(var + LN_EPS) * g.astype(jnp.float32)
            + b.astype(jnp.float32)).astype(z.dtype)


def pool_mixer(p, w_pool, s_pool):
    B, S, _ = p.shape
    pf = p.astype(jnp.float32)
    csum = jnp.cumsum(pf, axis=1)
    t = jnp.arange(S)
    outs = []
    for g, w in enumerate(POOL_WINDOWS):
        sl = slice(g * POOL_GROUP, (g + 1) * POOL_GROUP)
        cg = csum[..., sl]
        lower = jnp.pad(cg[:, :S - w], ((0, 0), (w, 0), (0, 0)))
        cnt = jnp.minimum(t + 1, w).astype(jnp.float32)[None, :, None]
        diff = ((cg - lower) / cnt - pf[..., sl]).astype(p.dtype)
        outs.append(diff @ w_pool[g])
    return jnp.concatenate(outs, axis=-1) * s_pool


def dilated_attention(q, k, v):
    B, S, _ = q.shape
    nb = S // Q_BLOCK
    starts = jnp.arange(nb) * Q_BLOCK
    scale = HEAD_DIM ** -0.5
    outs, lses = [], []
    for g, (window, dil) in enumerate(DIL_PATTERNS):
        sl = slice(g * DIL_GROUP_WIDTH, (g + 1) * DIL_GROUP_WIDTH)
        qg = q[..., sl].reshape(B, S, DIL_HEADS_PER_GROUP, HEAD_DIM)
        kg = k[..., sl].reshape(B, S, DIL_HEADS_PER_GROUP, HEAD_DIM)
        vg = v[..., sl].reshape(B, S, DIL_HEADS_PER_GROUP, HEAD_DIM)
        offs = dil * jnp.arange(window // dil + 1)

        def block(start, qg=qg, kg=kg, vg=vg, offs=offs):
            t = start + jnp.arange(Q_BLOCK)
            idx = t[:, None] - offs[None, :]
            valid = idx >= 0
            idx = jnp.maximum(idx, 0)
            qb = lax.dynamic_slice_in_dim(qg, start, Q_BLOCK, axis=1)
            kb = jnp.take(kg, idx, axis=1)
            vb = jnp.take(vg, idx, axis=1)
            s = jnp.einsum('bqhd,bqjhd->bhqj', qb, kb).astype(jnp.float32) * scale
            s = jnp.where(valid[None, None], s, NEG)
            lse = jax.nn.logsumexp(s, axis=-1)
            pr = jnp.exp(s - lse[..., None]).astype(vb.dtype)
            o = jnp.einsum('bhqj,bqjhd->bqhd', pr, vb)
            return o, lse

        o, lse = lax.map(block, starts)
        outs.append(o.transpose(1, 0, 2, 3, 4).reshape(B, S, DIL_HEADS_PER_GROUP, HEAD_DIM))
        lses.append(lse.transpose(1, 0, 3, 2).reshape(B, S, DIL_HEADS_PER_GROUP))
    alpha = jax.nn.softmax(jnp.stack(lses, axis=0), axis=0)
    merged = [outs[g] * alpha[g][..., None].astype(outs[g].dtype) for g in range(len(DIL_PATTERNS))]
    return jnp.concatenate(merged, axis=2).reshape(B, S, LOCAL_WIDTH)


def memory_attention(qm, mem, w_kv):
    B, S, _ = qm.shape
    kv = mem @ w_kv
    km = kv[..., :MEM_WIDTH].reshape(B, -1, MEM_HEADS, HEAD_DIM)
    vm = kv[..., MEM_WIDTH:].reshape(B, -1, MEM_HEADS, HEAD_DIM)
    qh = qm.reshape(B, S, MEM_HEADS, HEAD_DIM)
    s = jnp.einsum('bshd,bmhd->bhsm', qh, km).astype(jnp.float32) * (HEAD_DIM ** -0.5)
    pr = jax.nn.softmax(s, axis=-1).astype(vm.dtype)
    return jnp.einsum('bhsm,bmhd->bshd', pr, vm).reshape(B, S, MEM_WIDTH)


def peer_ffn(x, w_q, sub_keys, u, v):
    B, S, D = x.shape
    xt = x.reshape(-1, PEER_CHUNK, D)

    def chunk(xc):
        q = (xc @ w_q).reshape(PEER_CHUNK, PEER_HEADS, 2, PEER_HALF)
        s = jnp.einsum('chpk,hpnk->chpn', q, sub_keys).astype(jnp.float32)
        v1, i1 = lax.top_k(s[:, :, 0], PEER_TOPK)
        v2, i2 = lax.top_k(s[:, :, 1], PEER_TOPK)
        cand = (v1[..., :, None] + v2[..., None, :]).reshape(PEER_CHUNK, PEER_HEADS, PEER_TOPK * PEER_TOPK)
        cidx = (i1[..., :, None] * PEER_N_KEYS + i2[..., None, :]).reshape(PEER_CHUNK, PEER_HEADS, PEER_TOPK * PEER_TOPK)
        sv, sel = lax.top_k(cand, PEER_TOPK)
        eidx = jnp.take_along_axis(cidx, sel, axis=-1)
        gate = jax.nn.softmax(sv, axis=-1)
        u_sel = u[eidx]
        a = jnp.einsum('cd,chkd->chk', xc, u_sel).astype(jnp.float32)
        wgt = (gate * jax.nn.gelu(a, approximate=False)).astype(xc.dtype)
        v_sel = v[eidx]
        return jnp.einsum('chk,chkd->cd', wgt, v_sel)

    return lax.map(chunk, xt).reshape(B, S, D)


def setup_inputs(seed: int = 0) -> dict:
    key = jax.random.key(seed)
    ks = jax.random.split(key, 18)

    def nrm(k, shape, scale):
        return jax.random.normal(k, shape, jnp.float32) * scale

    s_in = D_MODEL ** -0.5
    beta = DEEPNORM_BETA
    x = nrm(ks[0], (BATCH, SEQ, D_MODEL), 1.0)
    mem = nrm(ks[1], (BATCH, N_MEM, D_MODEL), 1.0)
    w_in_a = jnp.concatenate([nrm(ks[2], (N_POOL_LAYERS, D_MODEL, LOCAL_WIDTH), s_in * beta),
                              nrm(ks[3], (N_POOL_LAYERS, D_MODEL, MEM_WIDTH), s_in)], axis=-1)
    w_pool = nrm(ks[4], (N_POOL_LAYERS, len(POOL_WINDOWS), POOL_GROUP, POOL_GROUP), POOL_GROUP ** -0.5)
    s_pool = 1.0 + nrm(ks[5], (N_POOL_LAYERS, LOCAL_WIDTH), 0.1)
    w_in_b = jnp.concatenate([nrm(ks[6], (N_DIL_LAYERS, D_MODEL, 2 * LOCAL_WIDTH), s_in),
                              nrm(ks[7], (N_DIL_LAYERS, D_MODEL, LOCAL_WIDTH), s_in * beta),
                              nrm(ks[8], (N_DIL_LAYERS, D_MODEL, MEM_WIDTH), s_in)], axis=-1)
    w_mem_kv = jnp.concatenate([nrm(ks[9], (DEPTH, D_MODEL, MEM_WIDTH), s_in),
                                nrm(ks[10], (DEPTH, D_MODEL, MEM_WIDTH), s_in * beta)], axis=-1)
    w_o = nrm(ks[11], (DEPTH, MIX_WIDTH, D_MODEL), MIX_WIDTH ** -0.5 * beta)
    ln_g = 1.0 + nrm(ks[12], (DEPTH, 2, D_MODEL), 0.05)
    ln_b = nrm(ks[13], (DEPTH, 2, D_MODEL), 0.02)
    peer_wq = nrm(ks[14], (DEPTH, D_MODEL, PEER_HEADS * PEER_QDIM), s_in)
    peer_keys = nrm(ks[15], (DEPTH, PEER_HEADS, 2, PEER_N_KEYS, PEER_HALF), PEER_HALF ** -0.5)
    peer_u = nrm(ks[16], (DEPTH, PEER_EXPERTS, D_MODEL), s_in)
    peer_v = nrm(ks[17], (DEPTH, PEER_EXPERTS, D_MODEL), beta * PEER_HEADS ** -0.5)
    return {"x": x, "mem": mem, "w_in_a": w_in_a, "w_pool": w_pool, "s_pool": s_pool,
            "w_in_b": w_in_b, "w_mem_kv": w_mem_kv, "w_o": w_o, "ln_g": ln_g, "ln_b": ln_b,
            "peer_wq": peer_wq, "peer_keys": peer_keys, "peer_u": peer_u, "peer_v": peer_v}


def reference(x, mem, w_in_a, w_pool, s_pool, w_in_b, w_mem_kv, w_o, ln_g, ln_b,
              peer_wq, peer_keys, peer_u, peer_v):
    ia = 0
    ib = 0
    for i in range(DEPTH):
        if i % N_MIXERS == 0:
            h = x @ w_in_a[ia]
            local = pool_mixer(h[..., :LOCAL_WIDTH], w_pool[ia], s_pool[ia])
            qm = h[..., LOCAL_WIDTH:]
            ia += 1
        else:
            h = x @ w_in_b[ib]
            q = h[..., :LOCAL_WIDTH]
            k = h[..., LOCAL_WIDTH:2 * LOCAL_WIDTH]
            v = h[..., 2 * LOCAL_WIDTH:3 * LOCAL_WIDTH]
            local = dilated_attention(q, k, v)
            qm = h[..., 3 * LOCAL_WIDTH:]
            ib += 1
        mem_out = memory_attention(qm, mem, w_mem_kv[i])
        mix = jnp.concatenate([local, mem_out], axis=-1) @ w_o[i]
        x = layer_norm(DEEPNORM_ALPHA * x + mix, ln_g[i, 0], ln_b[i, 0])
        ffn = peer_ffn(x, peer_wq[i], peer_keys[i], peer_u[i], peer_v[i])
        x = layer_norm(DEEPNORM_ALPHA * x + ffn, ln_g[i, 1], ln_b[i, 1])
    return x
```

```python
import functools

import jax
import jax.numpy as jnp
from jax import lax
from jax.experimental import pallas as pl
from jax.experimental.pallas import tpu as pltpu

f32 = jnp.float32
bf16 = jnp.bfloat16

D_MODEL = 2048
DEPTH = 4
HEAD_DIM = 128
MEM_HEADS = 4
MEM_WIDTH = MEM_HEADS * HEAD_DIM
LOCAL_WIDTH = D_MODEL - MEM_WIDTH
POOL_WINDOWS = (2, 4, 8, 16)
POOL_GROUP = LOCAL_WIDTH // len(POOL_WINDOWS)
DIL_PATTERNS = ((128, 1), (512, 4), (2048, 16))
DIL_HEADS = 4
DIL_GROUP_WIDTH = DIL_HEADS * HEAD_DIM
DIL_SPAN = 128
PEER_HEADS = 8
PEER_N_KEYS = 128
PEER_TOPK = 16
PEER_HALF = 128
DEEPNORM_ALPHA = (2 * DEPTH) ** 0.25
LN_EPS = 1e-5
NEG = -1e30
ATT_SCALE = HEAD_DIM ** -0.5
RSQRT2 = 0.7071067811865476

VMEM_LIMIT_BYTES = 56 * 1024 * 1024


def _params(*sem):
    return pltpu.CompilerParams(dimension_semantics=sem, vmem_limit_bytes=VMEM_LIMIT_BYTES)


def _dot_nt(a, b):
    return lax.dot_general(a, b, (((1,), (1,)), ((), ())), preferred_element_type=f32)


def _dot_tn(a, b):
    return lax.dot_general(a, b, (((0,), (0,)), ((), ())), preferred_element_type=f32)


def _layer_norm(z, g, b):
    mu = jnp.mean(z, axis=-1, keepdims=True)
    zc = z - mu
    var = jnp.mean(zc * zc, axis=-1, keepdims=True)
    return zc * lax.rsqrt(var + LN_EPS) * g + b


def _mm_kernel(x_ref, w_ref, o_ref):
    o_ref[...] = jnp.dot(x_ref[...], w_ref[...], preferred_element_type=f32).astype(o_ref.dtype)


def _matmul(x, w, out_dtype, tm=1024, tn=512):
    m, k = x.shape
    n = w.shape[1]
    return pl.pallas_call(
        _mm_kernel,
        grid=(m // tm, n // tn),
        in_specs=[pl.BlockSpec((tm, k), lambda i, j: (i, 0)),
                  pl.BlockSpec((k, tn), lambda i, j: (0, j))],
        out_specs=pl.BlockSpec((tm, tn), lambda i, j: (i, j)),
        out_shape=jax.ShapeDtypeStruct((m, n), out_dtype),
        compiler_params=_params("parallel", "parallel"),
        name="matmul",
    )(x, w)


POOL_ROWS = 1024
POOL_HALO = 16
POOL_CHUNK = 256


def _pool_kernel(cur_ref, halo_ref, w_ref, s_ref, o_ref, pad_ref):
    i = pl.program_id(1)
    pad_ref[POOL_HALO:, :] = cur_ref[0].astype(f32)
    halo = halo_ref[0].astype(f32)
    pad_ref[:POOL_HALO, :] = jnp.where(i > 0, halo, 0.0)
    for g, w in enumerate(POOL_WINDOWS):
        cols = slice(g * POOL_GROUP, (g + 1) * POOL_GROUP)
        for c in range(POOL_ROWS // POOL_CHUNK):
            r0 = POOL_HALO + c * POOL_CHUNK
            acc = pad_ref[r0:r0 + POOL_CHUNK, cols]
            p = acc
            for j in range(1, w):
                acc = acc + pad_ref[r0 - j:r0 - j + POOL_CHUNK, cols]
            t = i * POOL_ROWS + c * POOL_CHUNK + lax.broadcasted_iota(jnp.int32, (POOL_CHUNK, 1), 0)
            cnt = jnp.minimum(t + 1, w).astype(f32)
            diff = (acc / cnt - p).astype(bf16)
            y = jnp.dot(diff, w_ref[g], preferred_element_type=f32) * s_ref[:, cols]
            o_ref[0, c * POOL_CHUNK:(c + 1) * POOL_CHUNK, cols] = y.astype(o_ref.dtype)


def _pool_mixer(h3, w_pool, s_pool):
    b, s, _ = h3.shape
    blocks_per_step = POOL_ROWS // POOL_HALO
    return pl.pallas_call(
        _pool_kernel,
        grid=(b, s // POOL_ROWS),
        in_specs=[
            pl.BlockSpec((1, POOL_ROWS, LOCAL_WIDTH), lambda bb, i: (bb, i, 0)),
            pl.BlockSpec((1, POOL_HALO, LOCAL_WIDTH),
                         lambda bb, i: (bb, jnp.maximum(i * blocks_per_step - 1, 0), 0)),
            pl.BlockSpec((len(POOL_WINDOWS), POOL_GROUP, POOL_GROUP), lambda bb, i: (0, 0, 0)),
            pl.BlockSpec((1, LOCAL_WIDTH), lambda bb, i: (0, 0)),
        ],
        out_specs=pl.BlockSpec((1, POOL_ROWS, LOCAL_WIDTH), lambda bb, i: (bb, i, 0)),
        out_shape=jax.ShapeDtypeStruct((b, s, LOCAL_WIDTH), bf16),
        scratch_shapes=[pltpu.VMEM((POOL_HALO + POOL_ROWS, LOCAL_WIDTH), f32)],
        compiler_params=_params("parallel", "parallel"),
        name="pool_mixer",
    )(h3, h3, w_pool, s_pool)


MEM_TQ = 512


def _mem_attn_kernel(q_ref, kv_ref, o_ref):
    for hh in range(MEM_HEADS):
        cols = slice(hh * HEAD_DIM, (hh + 1) * HEAD_DIM)
        q = q_ref[0, :, cols]
        k = kv_ref[0, :, cols]
        v = kv_ref[0, :, MEM_WIDTH + hh * HEAD_DIM:MEM_WIDTH + (hh + 1) * HEAD_DIM]
        s = _dot_nt(q, k) * ATT_SCALE
        m = jnp.max(s, axis=-1, keepdims=True)
        p = jnp.exp(s - m)
        l = jnp.sum(p, axis=-1, keepdims=True)
        pr = (p / l).astype(bf16)
        o_ref[0, :, cols] = jnp.dot(pr, v, preferred_element_type=f32).astype(o_ref.dtype)


def _mem_attention(h3, q_col_block, kv3):
    b, s, _ = h3.shape
    n_mem = kv3.shape[1]
    return pl.pallas_call(
        _mem_attn_kernel,
        grid=(b, s // MEM_TQ),
        in_specs=[pl.BlockSpec((1, MEM_TQ, MEM_WIDTH), lambda bb, i: (bb, i, q_col_block)),
                  pl.BlockSpec((1, n_mem, 2 * MEM_WIDTH), lambda bb, i: (bb, 0, 0))],
        out_specs=pl.BlockSpec((1, MEM_TQ, MEM_WIDTH), lambda bb, i: (bb, i, 0)),
        out_shape=jax.ShapeDtypeStruct((b, s, MEM_WIDTH), bf16),
        compiler_params=_params("parallel", "parallel"),
        name="mem_attention",
    )(h3, kv3)


DIL_QB = 256


def _dil_attn_kernel(q_ref, kc_ref, kp_ref, vc_ref, vp_ref, o_ref, l_ref):
    i = pl.program_id(2)
    row = lax.broadcasted_iota(jnp.int32, (DIL_SPAN, DIL_SPAN), 0)
    col = lax.broadcasted_iota(jnp.int32, (DIL_SPAN, DIL_SPAN), 1)
    cur_mask = col <= row
    prev_tri = col >= row
    for hh in range(DIL_HEADS):
        cols = slice(hh * HEAD_DIM, (hh + 1) * HEAD_DIM)
        for j in range(DIL_QB // DIL_SPAN):
            rows = slice(j * DIL_SPAN, (j + 1) * DIL_SPAN)
            q = q_ref[0, rows, cols]
            kc = kc_ref[0, rows, cols]
            vc = vc_ref[0, rows, cols]
            if j == 0:
                kp = kp_ref[0, :, cols]
                vp = vp_ref[0, :, cols]
                prev_mask = prev_tri & (i > 0)
            else:
                prows = slice((j - 1) * DIL_SPAN, j * DIL_SPAN)
                kp = kc_ref[0, prows, cols]
                vp = vc_ref[0, prows, cols]
                prev_mask = prev_tri
            sc = jnp.where(cur_mask, _dot_nt(q, kc) * ATT_SCALE, NEG)
            sp = jnp.where(prev_mask, _dot_nt(q, kp) * ATT_SCALE, NEG)
            m = jnp.maximum(jnp.max(sc, axis=-1, keepdims=True), jnp.max(sp, axis=-1, keepdims=True))
            pc = jnp.exp(sc - m)
            pp = jnp.exp(sp - m)
            l = jnp.sum(pc, axis=-1, keepdims=True) + jnp.sum(pp, axis=-1, keepdims=True)
            inv = 1.0 / l
            o = (jnp.dot((pc * inv).astype(bf16), vc, preferred_element_type=f32)
                 + jnp.dot((pp * inv).astype(bf16), vp, preferred_element_type=f32))
            o_ref[0, rows, cols] = o.astype(o_ref.dtype)
            l_ref[0, rows, cols] = jnp.broadcast_to(m + jnp.log(l), (DIL_SPAN, HEAD_DIM))


def _dilated_group(h3, g, dil):
    b, s, nh = h3.shape
    r = s // dil
    hv = h3.reshape(b, r, dil * nh)
    cb = nh // DIL_GROUP_WIDTH
    kb = LOCAL_WIDTH // DIL_GROUP_WIDTH
    sub = DIL_QB // DIL_SPAN
    cur = lambda off: pl.BlockSpec((1, DIL_QB, DIL_GROUP_WIDTH),
                                   lambda bb, rr, i: (bb, i, rr * cb + off + g))
    prev = lambda off: pl.BlockSpec((1, DIL_SPAN, DIL_GROUP_WIDTH),
                                    lambda bb, rr, i: (bb, jnp.maximum(i * sub - 1, 0), rr * cb + off + g))
    out_spec = pl.BlockSpec((1, DIL_QB, DIL_GROUP_WIDTH), lambda bb, rr, i: (bb, i, rr))
    o, l = pl.pallas_call(
        _dil_attn_kernel,
        grid=(b, dil, r // DIL_QB),
        in_specs=[cur(0), cur(kb), prev(kb), cur(2 * kb), prev(2 * kb)],
        out_specs=[out_spec, out_spec],
        out_shape=[jax.ShapeDtypeStruct((b, r, dil * DIL_GROUP_WIDTH), bf16),
                   jax.ShapeDtypeStruct((b, r, dil * DIL_GROUP_WIDTH), f32)],
        compiler_params=_params("parallel", "parallel", "parallel"),
        name=f"dilated_attention_d{dil}",
    )(hv, hv, hv, hv, hv)
    return o.reshape(b * s, DIL_GROUP_WIDTH), l.reshape(b * s, DIL_GROUP_WIDTH)


MERGE_ROWS = 1024


def _merge_kernel(o0, o1, o2, l0, l1, l2, out_ref):
    ls = [l0[...], l1[...], l2[...]]
    m = jnp.maximum(jnp.maximum(ls[0], ls[1]), ls[2])
    es = [jnp.exp(l - m) for l in ls]
    inv = 1.0 / (es[0] + es[1] + es[2])
    for g, o in enumerate((o0, o1, o2)):
        out_ref[:, g * DIL_GROUP_WIDTH:(g + 1) * DIL_GROUP_WIDTH] = (
            o[...].astype(f32) * (es[g] * inv)).astype(out_ref.dtype)


def _merge_groups(os_, ls_):
    t = os_[0].shape[0]
    spec = pl.BlockSpec((MERGE_ROWS, DIL_GROUP_WIDTH), lambda i: (i, 0))
    return pl.pallas_call(
        _merge_kernel,
        grid=(t // MERGE_ROWS,),
        in_specs=[spec] * 6,
        out_specs=pl.BlockSpec((MERGE_ROWS, LOCAL_WIDTH), lambda i: (i, 0)),
        out_shape=jax.ShapeDtypeStruct((t, LOCAL_WIDTH), bf16),
        compiler_params=_params("parallel"),
        name="dilated_merge",
    )(*os_, *ls_)


WO_TM = 256


def _wo_ln_kernel(loc_ref, mem_ref, wl_ref, wm_ref, x_ref, g_ref, b_ref, o_ref, ob_ref):
    mix = (jnp.dot(loc_ref[...], wl_ref[...], preferred_element_type=f32)
           + jnp.dot(mem_ref[...], wm_ref[...], preferred_element_type=f32))
    y = _layer_norm(DEEPNORM_ALPHA * x_ref[...] + mix, g_ref[...], b_ref[...])
    o_ref[...] = y
    ob_ref[...] = y.astype(bf16)


def _wo_ln(local, mem_out, w_o, x, g, b):
    t = x.shape[0]
    row = lambda w: pl.BlockSpec((WO_TM, w), lambda i: (i, 0))
    full = lambda shape: pl.BlockSpec(shape, lambda i: (0, 0))
    return pl.pallas_call(
        _wo_ln_kernel,
        grid=(t // WO_TM,),
        in_specs=[row(LOCAL_WIDTH), row(MEM_WIDTH), full((LOCAL_WIDTH, D_MODEL)), full((MEM_WIDTH, D_MODEL)),
                  row(D_MODEL), full((1, D_MODEL)), full((1, D_MODEL))],
        out_specs=[row(D_MODEL), row(D_MODEL)],
        out_shape=[jax.ShapeDtypeStruct((t, D_MODEL), f32), jax.ShapeDtypeStruct((t, D_MODEL), bf16)],
        compiler_params=_params("parallel"),
        name="out_proj_layernorm",
    )(local, mem_out, w_o[:LOCAL_WIDTH], w_o[LOCAL_WIDTH:], x, g, b)


SEL_TC = 256
N_HALVES = 2 * PEER_HEADS
_CANDS = [(a, c) for a in range(PEER_TOPK) for c in range(PEER_TOPK) if (a + 1) * (c + 1) <= PEER_TOPK]


def _peer_select_kernel(q_ref, keys_ref, n1_ref, a1_ref, r2_ref, e2_ref,
                        qs_sc, s_sc, r_sc, v_sc, n_sc, zi_sc):
    tc = q_ref.shape[0]
    for hp in range(N_HALVES):
        qs_sc[hp] = q_ref[:, hp * PEER_HALF:(hp + 1) * PEER_HALF].astype(bf16)

    iota = lax.broadcasted_iota(jnp.int32, (PEER_N_KEYS, tc), 0)

    def rank_body(hp, carry):
        h = hp // 2
        p = hp % 2
        s = _dot_nt(keys_ref[hp], qs_sc[hp])
        s_sc[hp] = s
        work = s
        rank = jnp.full((PEER_N_KEYS, tc), float(PEER_N_KEYS - 1), f32)
        for r in range(PEER_TOPK):
            m = jnp.max(work, axis=0, keepdims=True)
            first = jnp.min(jnp.where(work == m, iota, PEER_N_KEYS), axis=0, keepdims=True)
            sel = iota == first
            rank = jnp.where(sel, float(r), rank)
            work = jnp.where(sel, -jnp.inf, work)
            v_sc[p, r, pl.ds(h, 1), :] = m
        r_sc[hp] = rank
        return carry

    lax.fori_loop(0, N_HALVES, rank_body, 0)

    v1 = [v_sc[0, r] for r in range(PEER_TOPK)]
    v2 = [v_sc[1, r] for r in range(PEER_TOPK)]
    e1 = [jnp.exp(v - v1[0]) for v in v1]
    e2 = [jnp.exp(v - v2[0]) for v in v2]
    sums = {c: v1[c[0]] + v2[c[1]] for c in _CANDS}
    cnt = {c: jnp.full(sums[c].shape, float((c[0] + 1) * (c[1] + 1) - 1), f32) for c in _CANDS}
    for ia, ca in enumerate(_CANDS):
        for cb in _CANDS[ia + 1:]:
            if ca[0] < cb[0] and ca[1] > cb[1]:
                a_first = sums[ca] >= sums[cb]
                cnt[cb] = cnt[cb] + jnp.where(a_first, 1.0, 0.0)
                cnt[ca] = cnt[ca] + jnp.where(a_first, 0.0, 1.0)
    z = jnp.zeros_like(v1[0])
    n = [jnp.zeros_like(v1[0]) for _ in range(PEER_TOPK)]
    for c in _CANDS:
        chosen = cnt[c] < float(PEER_TOPK)
        n[c[0]] = n[c[0]] + jnp.where(chosen, 1.0, 0.0)
        z = z + jnp.where(chosen, e1[c[0]] * e2[c[1]], 0.0)
    for r in range(PEER_TOPK):
        n_sc[r] = n[r]
    zi_sc[...] = 1.0 / z

    def emit_body(h, carry):
        row = pl.ds(h, 1)
        r1 = r_sc[2 * h]
        n1 = jnp.zeros((PEER_N_KEYS, tc), f32)
        for r in range(PEER_TOPK):
            n1 = jnp.where(r1 == float(r), n_sc[r, row, :], n1)
        n1_ref[h] = n1
        a1_ref[h] = jnp.exp(s_sc[2 * h] - v_sc[0, 0, row, :]) * zi_sc[row, :]
        r2_ref[h] = r_sc[2 * h + 1]
        e2_ref[h] = jnp.exp(s_sc[2 * h + 1] - v_sc[1, 0, row, :])
        return carry

    lax.fori_loop(0, PEER_HEADS, emit_body, 0)


def _peer_select(q, keys):
    t = q.shape[0]
    out_spec = pl.BlockSpec((PEER_HEADS, PEER_N_KEYS, SEL_TC), lambda i: (0, 0, i))
    out_shape = jax.ShapeDtypeStruct((PEER_HEADS, PEER_N_KEYS, t), f32)
    return pl.pallas_call(
        _peer_select_kernel,
        grid=(t // SEL_TC,),
        in_specs=[pl.BlockSpec((SEL_TC, q.shape[1]), lambda i: (i, 0)),
                  pl.BlockSpec(keys.shape, lambda i: (0, 0, 0))],
        out_specs=[out_spec] * 4,
        out_shape=[out_shape] * 4,
        scratch_shapes=[
            pltpu.VMEM((N_HALVES, SEL_TC, PEER_HALF), bf16),
            pltpu.VMEM((N_HALVES, PEER_N_KEYS, SEL_TC), f32),
            pltpu.VMEM((N_HALVES, PEER_N_KEYS, SEL_TC), f32),
            pltpu.VMEM((2, PEER_TOPK, PEER_HEADS, SEL_TC), f32),
            pltpu.VMEM((PEER_TOPK, PEER_HEADS, SEL_TC), f32),
            pltpu.VMEM((PEER_HEADS, SEL_TC), f32),
        ],
        compiler_params=_params("parallel"),
        name="peer_select",
    )(q, keys)


PEER_TC = 512
PEER_TE = 512


def _peer_dense_kernel(xb_ref, u_ref, v_ref, n1_ref, a1_ref, r2_ref, e2_ref, x_ref, g_ref, b_ref,
                       o_ref, ob_ref, h_sc):
    e = pl.program_id(1)
    rows_per_step = PEER_TE // PEER_N_KEYS

    @pl.when(e == 0)
    def _():
        o_ref[...] = jnp.zeros_like(o_ref)

    a_t = _dot_nt(u_ref[...], xb_ref[...])
    for il in range(rows_per_step):
        i1 = e * rows_per_step + il
        rows = slice(il * PEER_N_KEYS, (il + 1) * PEER_N_KEYS)
        a = a_t[rows, :]
        w = jnp.zeros(a.shape, f32)
        for h in range(PEER_HEADS):
            n1 = n1_ref[h, pl.ds(i1, 1), :]
            a1 = a1_ref[h, pl.ds(i1, 1), :]
            w = w + jnp.where(r2_ref[h] < n1, a1 * e2_ref[h], 0.0)
        gelu = 0.5 * a * (1.0 + lax.erf(a * RSQRT2))
        h_sc[rows, :] = (w * gelu).astype(bf16)
    o_ref[...] += _dot_tn(h_sc[...], v_ref[...])

    @pl.when(e == pl.num_programs(1) - 1)
    def _():
        y = _layer_norm(DEEPNORM_ALPHA * x_ref[...] + o_ref[...], g_ref[...], b_ref[...])
        o_ref[...] = y
        ob_ref[...] = y.astype(bf16)


def _peer_dense(xb, u, v, sel, x, g, b):
    t = x.shape[0]
    n_exp = u.shape[0]
    tok = lambda w: pl.BlockSpec((PEER_TC, w), lambda i, e: (i, 0))
    tab = pl.BlockSpec((PEER_TE, D_MODEL), lambda i, e: (e, 0))
    sel_spec = pl.BlockSpec((PEER_HEADS, PEER_N_KEYS, PEER_TC), lambda i, e: (0, 0, i))
    vec = pl.BlockSpec((1, D_MODEL), lambda i, e: (0, 0))
    return pl.pallas_call(
        _peer_dense_kernel,
        grid=(t // PEER_TC, n_exp // PEER_TE),
        in_specs=[tok(D_MODEL), tab, tab, sel_spec, sel_spec, sel_spec, sel_spec, tok(D_MODEL), vec, vec],
        out_specs=[tok(D_MODEL), tok(D_MODEL)],
        out_shape=[jax.ShapeDtypeStruct((t, D_MODEL), f32), jax.ShapeDtypeStruct((t, D_MODEL), bf16)],
        scratch_shapes=[pltpu.VMEM((PEER_TE, PEER_TC), bf16)],
        compiler_params=_params("parallel", "arbitrary"),
        name="peer_dense",
    )(xb, u, v, *sel, x, g, b)


def kernel(x, mem, w_in_a, w_pool, s_pool, w_in_b, w_mem_kv, w_o, ln_g, ln_b, peer_wq, peer_keys, peer_u, peer_v):
    b, s, d = x.shape
    t = b * s
    n_mem = mem.shape[1]
    xf = x.reshape(t, d)
    xb = xf.astype(bf16)
    memb = mem.reshape(b * n_mem, d).astype(bf16)
    ia = ib = 0
    for i in range(DEPTH):
        if i % 2 == 0:
            nh = LOCAL_WIDTH + MEM_WIDTH
            h3 = _matmul(xb, w_in_a[ia].astype(bf16), bf16).reshape(b, s, nh)
            local = _pool_mixer(h3, w_pool[ia].astype(bf16), s_pool[ia].reshape(1, LOCAL_WIDTH)).reshape(t, LOCAL_WIDTH)
            ia += 1
        else:
            nh = 3 * LOCAL_WIDTH + MEM_WIDTH
            h3 = _matmul(xb, w_in_b[ib].astype(bf16), bf16).reshape(b, s, nh)
            outs = [_dilated_group(h3, g, dil) for g, (_, dil) in enumerate(DIL_PATTERNS)]
            local = _merge_groups([o for o, _ in outs], [l for _, l in outs])
            ib += 1
        kv3 = _matmul(memb, w_mem_kv[i].astype(bf16), bf16, tm=b * n_mem // 2).reshape(b, n_mem, 2 * MEM_WIDTH)
        mem_out = _mem_attention(h3, (nh - MEM_WIDTH) // MEM_WIDTH, kv3).reshape(t, MEM_WIDTH)
        xf, xb = _wo_ln(local, mem_out, w_o[i].astype(bf16), xf,
                        ln_g[i, 0].reshape(1, d), ln_b[i, 0].reshape(1, d))
        q = _matmul(xb, peer_wq[i].astype(bf16), f32)
        keys = peer_keys[i].reshape(N_HALVES, PEER_N_KEYS, PEER_HALF).astype(bf16)
        sel = _peer_select(q, keys)
        xf, xb = _peer_dense(xb, peer_u[i].astype(bf16), peer_v[i].astype(bf16), sel, xf,
                             ln_g[i, 1].reshape(1, d), ln_b[i, 1].reshape(1, d))
    return xf.reshape(b, s, d)
```
